```python
import math
import jax, jax.numpy as jnp
from jax import lax
import numpy as np

D_MODEL = 1024
BATCH = 4
SEQ = 8192
DEPTH = 1

N_META = 16
EXPAND = 2
D_MIX = EXPAND * D_MODEL
D_SSD = D_MIX // 2
SSD_HEADDIM = 64
SSD_HEADS = D_SSD // SSD_HEADDIM
SSD_GROUPS = 2
D_STATE = 128
CONV_K = 4
CHUNK = 128
D_ATTN = D_MIX - D_SSD
DIFF_HEADS = 8
DIFF_V_DIM = D_ATTN // DIFF_HEADS
DIFF_QK_DIM = DIFF_V_DIM // 2
Q_BLOCK = CHUNK
CONV_DIM = D_SSD + 2 * SSD_GROUPS * D_STATE
SPLIT_SIZES = (D_SSD, CONV_DIM, SSD_HEADS, D_ATTN, D_ATTN, D_ATTN, D_ATTN)
D_PROJ = sum(SPLIT_SIZES)
N_PAD = CHUNK - N_META
EPS = 1e-5
NEG_INF = -1e30
ALIBI_MAX = 8.0
DT_MIN = 1e-3
DT_MAX = 1e-1

kernel_name = "hymba_ssd_diffattn_alibi_layer"


def rmsnorm(x, g):
    xf = x.astype(jnp.float32)
    y = xf * lax.rsqrt(jnp.mean(xf * xf, axis=-1, keepdims=True) + EPS)
    return (y * g.astype(jnp.float32)).astype(x.dtype)


def left_pad(t, n):
    return jnp.pad(t, [(0, 0), (n, 0)] + [(0, 0)] * (t.ndim - 2))


def causal_dwconv(x, w, b):
    y = lax.conv_general_dilated(
        x, w[:, None, :], window_strides=(1,), padding=[(CONV_K - 1, 0)],
        dimension_numbers=("NWC", "WIO", "NWC"), feature_group_count=x.shape[-1])
    return y + b


def ssd_chunked(xs, dt, a, bmat, cmat):
    b, lp = xs.shape[:2]
    nc = lp // CHUNK
    r = SSD_HEADS // SSD_GROUPS
    xdt = (xs * dt[..., None]).reshape(b, nc, CHUNK, SSD_GROUPS, r, SSD_HEADDIM)
    adt = (dt * a).reshape(b, nc, CHUNK, SSD_GROUPS, r).transpose(0, 1, 3, 4, 2)
    bc = bmat.reshape(b, nc, CHUNK, SSD_GROUPS, D_STATE)
    cc = cmat.reshape(b, nc, CHUNK, SSD_GROUPS, D_STATE)
    acs = jnp.cumsum(adt, axis=-1)
    causal = jnp.tril(jnp.ones((CHUNK, CHUNK), dtype=bool))
    seg = jnp.exp(jnp.where(causal, acs[..., :, None] - acs[..., None, :], -jnp.inf))
    cb = jnp.einsum("bclgn,bcsgn->bcgls", cc, bc)
    y_diag = jnp.einsum("bcgrls,bcsgrp->bclgrp", cb[:, :, :, None] * seg, xdt)
    decay_to_end = jnp.exp(acs[..., -1:] - acs)
    chunk_states = jnp.einsum("bclgn,bcgrl,bclgrp->bcgrpn", bc, decay_to_end, xdt)
    chunk_decay = jnp.exp(acs[..., -1])

    def step(state, inp):
        st, dec = inp
        return dec[..., None, None] * state + st, state

    init = jnp.zeros_like(chunk_states[:, 0])
    _, prev = lax.scan(step, init, (jnp.moveaxis(chunk_states, 1, 0), jnp.moveaxis(chunk_decay, 1, 0)))
    prev = jnp.moveaxis(prev, 0, 1)
    y_off = jnp.einsum("bclgn,bcgrpn,bcgrl->bclgrp", cc, prev, jnp.exp(acs))
    return (y_diag + y_off).reshape(b, lp, SSD_HEADS, SSD_HEADDIM)


def diff_attention(q, k, v, lam, slopes):
    b, lp, _ = q.shape
    q = q.reshape(b, lp, DIFF_HEADS, 2, DIFF_QK_DIM).transpose(0, 2, 3, 1, 4)
    k = k.reshape(b, lp, DIFF_HEADS, 2, DIFF_QK_DIM).transpose(0, 2, 3, 1, 4)
    v = v.reshape(b, lp, DIFF_HEADS, DIFF_V_DIM).transpose(0, 2, 1, 3)
    scale = DIFF_QK_DIM ** -0.5
    outs = []
    for i in range(lp // Q_BLOCK):
        q_lo, kv_hi = i * Q_BLOCK, (i + 1) * Q_BLOCK
        qb = q[:, :, :, q_lo:kv_hi]
        kb = k[:, :, :, :kv_hi]
        vb = v[:, :, :kv_hi]
        s = jnp.einsum("bhjqd,bhjkd->bhjqk", qb, kb).astype(jnp.float32) * scale
        qpos = q_lo + jnp.arange(Q_BLOCK)[:, None]
        kpos = jnp.arange(kv_hi)[None, :]
        dist = (qpos - kpos).astype(jnp.float32)
        allowed = (kpos <= qpos) & (kpos >= N_PAD)
        s = s - slopes[None, :, None, None, None] * dist
        s = jnp.where(allowed, s, NEG_INF)
        p = jax.nn.softmax(s, axis=-1)
        w = p[:, :, 0] - lam * p[:, :, 1]
        outs.append(jnp.einsum("bhqk,bhke->bhqe", w.astype(v.dtype), vb))
    o = jnp.concatenate(outs, axis=2)
    return o.transpose(0, 2, 1, 3)


def setup_inputs(seed: int = 0) -> dict:
    key = jax.random.key(seed)
    ks = jax.random.split(key, 17)
    f32 = jnp.float32
    nrm = jax.random.normal
    x = nrm(ks[0], (BATCH, SEQ, D_MODEL), f32)
    meta = nrm(ks[1], (N_META, D_MODEL), f32)
    norm_g = 1.0 + 0.05 * nrm(ks[2], (DEPTH, D_MODEL), f32)
    w_in = nrm(ks[3], (DEPTH, D_MODEL, D_PROJ), f32) * D_MODEL ** -0.5
    conv_w = nrm(ks[4], (DEPTH, CONV_K, CONV_DIM), f32) * CONV_K ** -0.5
    conv_b = 0.02 * nrm(ks[5], (DEPTH, CONV_DIM), f32)
    u = jax.random.uniform(ks[6], (DEPTH, SSD_HEADS), f32)
    dt0 = jnp.exp(u * (math.log(DT_MAX) - math.log(DT_MIN)) + math.log(DT_MIN))
    dt_bias = dt0 + jnp.log(-jnp.expm1(-dt0))
    a_log = jnp.log(jax.random.uniform(ks[7], (DEPTH, SSD_HEADS), f32, minval=1.0, maxval=16.0))
    d_skip = 1.0 + 0.1 * nrm(ks[8], (DEPTH, SSD_HEADS), f32)
    ssd_norm_g = 1.0 + 0.05 * nrm(ks[9], (DEPTH, D_SSD), f32)
    lambda_q1 = 0.1 * nrm(ks[10], (DEPTH, DIFF_QK_DIM), f32)
    lambda_k1 = 0.1 * nrm(ks[11], (DEPTH, DIFF_QK_DIM), f32)
    lambda_q2 = 0.1 * nrm(ks[12], (DEPTH, DIFF_QK_DIM), f32)
    lambda_k2 = 0.1 * nrm(ks[13], (DEPTH, DIFF_QK_DIM), f32)
    attn_norm_g = 1.0 + 0.05 * nrm(ks[14], (DEPTH, DIFF_V_DIM), f32)
    w_out = nrm(ks[15], (DEPTH, D_MIX, D_MODEL), f32) * D_MIX ** -0.5
    final_norm_g = 1.0 + 0.05 * nrm(ks[16], (D_MODEL,), f32)
    return {"x": x, "meta": meta, "norm_g": norm_g, "w_in": w_in, "conv_w": conv_w,
            "conv_b": conv_b, "dt_bias": dt_bias, "a_log": a_log, "d_skip": d_skip,
            "ssd_norm_g": ssd_norm_g, "lambda_q1": lambda_q1, "lambda_k1": lambda_k1,
            "lambda_q2": lambda_q2, "lambda_k2": lambda_k2, "attn_norm_g": attn_norm_g,
            "w_out": w_out, "final_norm_g": final_norm_g}


def reference(x, meta, norm_g, w_in, conv_w, conv_b, dt_bias, a_log, d_skip, ssd_norm_g,
              lambda_q1, lambda_k1, lambda_q2, lambda_k2, attn_norm_g, w_out, final_norm_g):
    b = x.shape[0]
    h = jnp.concatenate([jnp.broadcast_to(meta[None], (b, N_META, D_MODEL)).astype(x.dtype), x], axis=1)
    L = h.shape[1]
    split_points = list(np.cumsum(SPLIT_SIZES)[:-1])
    slopes = jnp.exp2(-ALIBI_MAX * jnp.arange(1, DIFF_HEADS + 1, dtype=jnp.float32) / DIFF_HEADS)
    for layer in range(DEPTH):
        hn = rmsnorm(h, norm_g[layer])
        proj = hn @ w_in[layer]
        z_ssd, xbc, dt_raw, q, k, v, z_attn = jnp.split(proj, split_points, axis=-1)

        xbc = jax.nn.silu(causal_dwconv(xbc, conv_w[layer], conv_b[layer])).astype(jnp.float32)
        dt = jax.nn.softplus(dt_raw.astype(jnp.float32) + dt_bias[layer].astype(jnp.float32))
        a = -jnp.exp(a_log[layer].astype(jnp.float32))
        xbc_p = left_pad(xbc, N_PAD)
        dt_p = left_pad(dt, N_PAD)
        lp = L + N_PAD
        xs = xbc_p[..., :D_SSD].reshape(b, lp, SSD_HEADS, SSD_HEADDIM)
        bmat = xbc_p[..., D_SSD:D_SSD + SSD_GROUPS * D_STATE].reshape(b, lp, SSD_GROUPS, D_STATE)
        cmat = xbc_p[..., D_SSD + SSD_GROUPS * D_STATE:].reshape(b, lp, SSD_GROUPS, D_STATE)
        y = ssd_chunked(xs, dt_p, a, bmat, cmat)
        y = y + d_skip[layer].astype(jnp.float32)[:, None] * xs
        y = y[:, N_PAD:].reshape(b, L, D_SSD)
        y = y * jax.nn.silu(z_ssd.astype(jnp.float32))
        yg = y.reshape(b, L, SSD_GROUPS, D_SSD // SSD_GROUPS)
        yg = yg * lax.rsqrt(jnp.mean(yg * yg, axis=-1, keepdims=True) + EPS)
        y_ssd = (yg.reshape(b, L, D_SSD) * ssd_norm_g[layer].astype(jnp.float32)).astype(h.dtype)

        lam_init = 0.8 - 0.6 * math.exp(-0.3 * layer)
        lam = (jnp.exp(jnp.sum(lambda_q1[layer].astype(jnp.float32) * lambda_k1[layer].astype(jnp.float32)))
               - jnp.exp(jnp.sum(lambda_q2[layer].astype(jnp.float32) * lambda_k2[layer].astype(jnp.float32)))
               + lam_init)
        o = diff_attention(left_pad(q, N_PAD), left_pad(k, N_PAD), left_pad(v, N_PAD), lam, slopes)
        o = o[:, N_PAD:]
        o = rmsnorm(o, attn_norm_g[layer]).astype(jnp.float32) * (1.0 - lam_init)
        y_attn = (o.reshape(b, L, D_ATTN) * jax.nn.silu(z_attn.astype(jnp.float32))).astype(h.dtype)

        h = h + jnp.concatenate([y_ssd, y_attn], axis=-1) @ w_out[layer]
    return rmsnorm(h[:, N_META:], final_norm_g)
```

```python
import functools
import math

import jax
import jax.numpy as jnp
from jax import lax
from jax.experimental import pallas as pl
from jax.experimental.pallas import tpu as pltpu

D_MODEL = 1024
N_META = 16
D_SSD = 1024
SSD_HEADDIM = 64
SSD_HEADS = 16
SSD_GROUPS = 2
D_STATE = 128
CONV_K = 4
CHUNK = 128
D_ATTN = 1024
DIFF_HEADS = 8
DIFF_V_DIM = 128
DIFF_QK_DIM = 64
CONV_DIM = D_SSD + 2 * SSD_GROUPS * D_STATE
N_PAD = CHUNK - N_META
EPS = 1e-5
NEG_INF = -1e30
ALIBI_MAX = 8.0
LAM_INIT = 0.8 - 0.6 * math.exp(-0.3 * 0)

LANES = 128
DT_PAD = LANES
HALO = 8
VMEM_LIMIT = 56 * 1024 * 1024

F32 = jnp.float32
BF16 = jnp.bfloat16


def _silu(v):
    return v * jax.nn.sigmoid(v)


def _proj_kernel(x_ref, g_ref, wzs_ref, wxbc_ref, wdt_ref, wq_ref, wk_ref, wv_ref, wza_ref,
                 zs_ref, xbc_ref, dt_ref, q_ref, k_ref, v_ref, za_ref):
    x = x_ref[...]
    ms = jnp.mean(x * x, axis=-1, keepdims=True)
    hn = (x * lax.rsqrt(ms + EPS) * g_ref[...]).astype(BF16)

    def mm(w_ref):
        return jnp.dot(hn, w_ref[...], preferred_element_type=F32)

    zs_ref[...] = mm(wzs_ref)
    xbc_ref[...] = mm(wxbc_ref)
    dt_ref[...] = mm(wdt_ref)
    q_ref[...] = mm(wq_ref).astype(BF16)
    k_ref[...] = mm(wk_ref).astype(BF16)
    v_ref[...] = mm(wv_ref).astype(BF16)
    za_ref[...] = mm(wza_ref)


def _proj(x2, g, ws, tm):
    m = x2.shape[0]
    widths = [w.shape[1] for w in ws]
    dtypes = [F32, F32, F32, BF16, BF16, BF16, F32]
    const = lambda i: (0, 0)
    row = lambda i: (i, 0)
    in_specs = [pl.BlockSpec((tm, D_MODEL), row), pl.BlockSpec((1, D_MODEL), const)]
    in_specs += [pl.BlockSpec((D_MODEL, n), const, pipeline_mode=pl.Buffered(1)) for n in widths]
    out_specs = [pl.BlockSpec((tm, n), row) for n in widths]
    out_shape = [jax.ShapeDtypeStruct((m, n), dt) for n, dt in zip(widths, dtypes)]
    return pl.pallas_call(
        _proj_kernel,
        grid=(m // tm,),
        in_specs=in_specs,
        out_specs=out_specs,
        out_shape=out_shape,
        compiler_params=pltpu.CompilerParams(
            dimension_semantics=("arbitrary",), vmem_limit_bytes=VMEM_LIMIT),
        name="proj",
    )(x2, g, *ws)


def _conv_silu(xs_scr, cw_ref, cb_ref):
    acc = cb_ref[...] + xs_scr[pl.ds(HALO - 3, CHUNK), :] * cw_ref[0:1, :]
    for k in range(1, CONV_K):
        acc = acc + xs_scr[pl.ds(HALO - 3 + k, CHUNK), :] * cw_ref[k:k + 1, :]
    return _silu(acc)


def _softplus(v):
    return jnp.maximum(v, 0.0) + jnp.log1p(jnp.exp(-jnp.abs(v)))


def _pair_expand(v, j, lane_lo):
    a = jnp.broadcast_to(v[:, 2 * j:2 * j + 1], (v.shape[0], LANES))
    b = jnp.broadcast_to(v[:, 2 * j + 1:2 * j + 2], (v.shape[0], LANES))
    return jnp.where(lane_lo, a, b)


def _ssd_chunk(xbc, dt, a_row, s_scr, *, want_y, dskip_ref=None):
    n_pairs = SSD_HEADS // 2
    pairs_per_group = n_pairs // SSD_GROUPS
    gw = pairs_per_group * LANES

    row_i = lax.broadcasted_iota(jnp.int32, (CHUNK, CHUNK), 0)
    col_i = lax.broadcasted_iota(jnp.int32, (CHUNK, CHUNK), 1)
    tri = row_i >= col_i
    lane_lo = col_i < SSD_HEADDIM

    adt = dt * a_row
    acs = jnp.dot(tri.astype(F32), adt, precision=lax.Precision.HIGHEST,
                  preferred_element_type=F32)
    acs_t = acs.T
    eacs = jnp.exp(acs)
    decay_end = jnp.exp(acs[CHUNK - 1:CHUNK, :] - acs)

    ys = []
    for g in range(SSD_GROUPS):
        bg = xbc[:, D_SSD + g * D_STATE:D_SSD + (g + 1) * D_STATE]
        s_prev = s_scr[g]
        xdec = []
        if want_y:
            cg = xbc[:, D_SSD + (SSD_GROUPS + g) * D_STATE:D_SSD + (SSD_GROUPS + g + 1) * D_STATE].astype(BF16)
            cb = lax.dot_general(cg, bg.astype(BF16), (((1,), (1,)), ((), ())),
                                 preferred_element_type=F32)
            y_off = jnp.dot(cg, s_prev.astype(BF16), preferred_element_type=F32)
        for jj in range(pairs_per_group):
            j = g * pairs_per_group + jj
            xp = xbc[:, j * LANES:(j + 1) * LANES]
            xdt = xp * _pair_expand(dt, j, lane_lo)
            xdec.append((xdt * _pair_expand(decay_end, j, lane_lo)).astype(BF16))
            if want_y:
                ms = []
                for h in (2 * j, 2 * j + 1):
                    seg = jnp.exp(jnp.where(tri, acs[:, h:h + 1] - acs_t[h:h + 1, :], -jnp.inf))
                    ms.append((cb * seg).astype(BF16))
                lhs = jnp.concatenate(ms, axis=1)
                rhs = jnp.concatenate([jnp.where(lane_lo, xdt, 0.0), jnp.where(lane_lo, 0.0, xdt)],
                                      axis=0).astype(BF16)
                y_diag = jnp.dot(lhs, rhs, preferred_element_type=F32)
                y = (y_diag + y_off[:, jj * LANES:(jj + 1) * LANES] * _pair_expand(eacs, j, lane_lo)
                     + dskip_ref[:, j * LANES:(j + 1) * LANES] * xp)
                ys.append(y)
        chunk_decay = jnp.concatenate(
            [_pair_expand(eacs, g * pairs_per_group + jj, lane_lo)[CHUNK - 1:CHUNK, :]
             for jj in range(pairs_per_group)], axis=1)
        bg_t = bg.T.astype(BF16)
        s_new = jnp.dot(bg_t, jnp.concatenate(xdec, axis=1), preferred_element_type=F32)
        s_scr[g] = chunk_decay * s_prev + s_new
    return ys


def _ssd_kernel(xbc_ref, dt_ref, z_ref, mxbc_ref, mdt_ref, cw_ref, cb_ref, dtb_ref, alog_ref,
                dskip_ref, g_ref, y_ref, xs_scr, s_scr):
    c = pl.program_id(1)
    a_row = -jnp.exp(alog_ref[...])

    @pl.when(c == 0)
    def _():
        xs_scr[pl.ds(0, HALO), :] = jnp.zeros((HALO, CONV_DIM), F32)
        xs_scr[pl.ds(HALO, CHUNK), :] = mxbc_ref[...]
        valid = lax.broadcasted_iota(jnp.int32, (CHUNK, 1), 0) >= N_PAD
        xbc = jnp.where(valid, _conv_silu(xs_scr, cw_ref, cb_ref), 0.0)
        dt = jnp.where(valid, _softplus(mdt_ref[...] + dtb_ref[...]), 0.0)
        s_scr[...] = jnp.zeros(s_scr.shape, F32)
        _ssd_chunk(xbc, dt, a_row, s_scr, want_y=False)
        xs_scr[pl.ds(0, HALO), :] = xs_scr[pl.ds(CHUNK, HALO), :]

    xs_scr[pl.ds(HALO, CHUNK), :] = xbc_ref[...]
    xbc = _conv_silu(xs_scr, cw_ref, cb_ref)
    dt = _softplus(dt_ref[...] + dtb_ref[...])
    ys = _ssd_chunk(xbc, dt, a_row, s_scr, want_y=True, dskip_ref=dskip_ref)
    xs_scr[pl.ds(0, HALO), :] = xs_scr[pl.ds(CHUNK, HALO), :]

    gw = D_SSD // SSD_GROUPS
    pairs_per_group = gw // LANES
    for g in range(SSD_GROUPS):
        yg = jnp.concatenate(ys[g * pairs_per_group:(g + 1) * pairs_per_group], axis=1)
        yg = yg * _silu(z_ref[:, g * gw:(g + 1) * gw])
        yg = yg * lax.rsqrt(jnp.mean(yg * yg, axis=-1, keepdims=True) + EPS)
        y_ref[:, g * gw:(g + 1) * gw] = (yg * g_ref[:, g * gw:(g + 1) * gw]).astype(BF16)


def _ssd(xbc, dt, zs, mxbc, mdt, cw, cb, dtb, alog, dskip, gn, batch, nc):
    m = xbc.shape[0]
    const = lambda b, c: (0, 0)
    row = lambda b, c: (b * nc + c, 0)
    in_specs = [
        pl.BlockSpec((CHUNK, CONV_DIM), row),
        pl.BlockSpec((CHUNK, DT_PAD), row),
        pl.BlockSpec((CHUNK, D_SSD), row),
        pl.BlockSpec((CHUNK, CONV_DIM), const),
        pl.BlockSpec((CHUNK, DT_PAD), const),
        pl.BlockSpec((CONV_K, CONV_DIM), const),
        pl.BlockSpec((1, CONV_DIM), const),
        pl.BlockSpec((1, DT_PAD), const),
        pl.BlockSpec((1, DT_PAD), const),
        pl.BlockSpec((1, D_SSD), const),
        pl.BlockSpec((1, D_SSD), const),
    ]
    return pl.pallas_call(
        _ssd_kernel,
        grid=(batch, nc),
        in_specs=in_specs,
        out_specs=pl.BlockSpec((CHUNK, D_SSD), row),
        out_shape=jax.ShapeDtypeStruct((m, D_SSD), BF16),
        scratch_shapes=[
            pltpu.VMEM((HALO + CHUNK, CONV_DIM), F32),
            pltpu.VMEM((SSD_GROUPS, D_STATE, D_SSD // SSD_GROUPS), F32),
        ],
        compiler_params=pltpu.CompilerParams(
            dimension_semantics=("arbitrary", "arbitrary"), vmem_limit_bytes=VMEM_LIMIT),
        name="ssd",
    )(xbc, dt, zs, mxbc, mdt, cw, cb, dtb, alog, dskip, gn)


def _attn_kernel(slopes_ref, q_ref, k_ref, v_ref, km_ref, vm_ref, za_ref,
                 lq1_ref, lk1_ref, lq2_ref, lk2_ref, g_ref, o_ref, acc_scr, m_scr, *, tq):
    h = pl.program_id(1)
    qi = pl.program_id(2)
    slope = slopes_ref[h]
    q0 = qi * tq
    tk = tq

    qf = q_ref[...].astype(F32)
    lane = lax.broadcasted_iota(jnp.int32, (tq, LANES), 1)
    qs = jnp.concatenate([jnp.where(lane < DIFF_QK_DIM, qf, 0.0),
                          jnp.where(lane < DIFF_QK_DIM, 0.0, qf)], axis=0).astype(BF16)

    acc_scr[...] = jnp.zeros(acc_scr.shape, F32)
    m_scr[...] = jnp.full(m_scr.shape, -jnp.inf, F32)

    def step(k_t, v_t, bias_row, mask):
        s = lax.dot_general(qs, k_t, (((1,), (1,)), ((), ())), preferred_element_type=F32)
        s = s + bias_row
        if mask is not None:
            s = jnp.where(mask, s, NEG_INF)
        m_prev = m_scr[...]
        m_new = jnp.maximum(m_prev, jnp.max(s, axis=1, keepdims=True))
        alpha = jnp.exp(m_prev - m_new)
        p = jnp.exp(s - m_new).astype(BF16)
        v_aug = jnp.concatenate([v_t, jnp.ones_like(v_t)], axis=1)
        acc_scr[...] = alpha * acc_scr[...] + jnp.dot(p, v_aug, preferred_element_type=F32)
        m_scr[...] = m_new

    cm = lax.broadcasted_iota(jnp.int32, (1, CHUNK), 1)
    bias_m = jnp.where(cm >= N_PAD, slope * (cm - CHUNK - q0).astype(F32), NEG_INF)
    step(km_ref[...], vm_ref[...], bias_m, None)

    ck = lax.broadcasted_iota(jnp.int32, (1, tk), 1)

    def body(j, carry):
        k0 = pl.multiple_of(j * tk, tk)
        bias = slope * (ck + (k0 - q0)).astype(F32)
        step(k_ref[pl.ds(k0, tk), :], v_ref[pl.ds(k0, tk), :], bias, None)
        return carry

    lax.fori_loop(0, qi, body, 0)

    k0 = pl.multiple_of(q0, tk)
    r = lax.broadcasted_iota(jnp.int32, (2 * tq, tk), 0)
    cc = lax.broadcasted_iota(jnp.int32, (2 * tq, tk), 1)
    causal = cc <= jnp.where(r >= tq, r - tq, r)
    step(k_ref[pl.ds(k0, tk), :], v_ref[pl.ds(k0, tk), :], slope * ck.astype(F32), causal)

    acc = acc_scr[...]
    o1 = acc[:tq, :DIFF_V_DIM] / acc[:tq, DIFF_V_DIM:]
    o2 = acc[tq:, :DIFF_V_DIM] / acc[tq:, DIFF_V_DIM:]
    lam = (jnp.exp(jnp.sum(lq1_ref[...] * lk1_ref[...], axis=-1, keepdims=True))
           - jnp.exp(jnp.sum(lq2_ref[...] * lk2_ref[...], axis=-1, keepdims=True)) + LAM_INIT)
    o = o1 - lam * o2
    o = o * lax.rsqrt(jnp.mean(o * o, axis=-1, keepdims=True) + EPS) * g_ref[...]
    o = o * (1.0 - LAM_INIT)
    o_ref[...] = (o * _silu(za_ref[...])).astype(BF16)


def _attn(slopes, q, k, v, km, vm, za, lq1, lk1, lq2, lk2, gn, batch, seq, tq):
    nq = seq // tq
    m = q.shape[0]
    qmap = lambda b, h, i, *_: (b * nq + i, h)
    kvmap = lambda b, h, i, *_: (b, h)
    hmap = lambda b, h, i, *_: (0, h)
    const = lambda b, h, i, *_: (0, 0)
    in_specs = [
        pl.BlockSpec((tq, LANES), qmap),
        pl.BlockSpec((seq, LANES), kvmap),
        pl.BlockSpec((seq, LANES), kvmap),
        pl.BlockSpec((CHUNK, LANES), hmap),
        pl.BlockSpec((CHUNK, LANES), hmap),
        pl.BlockSpec((tq, LANES), qmap),
        pl.BlockSpec((1, DIFF_QK_DIM), const),
        pl.BlockSpec((1, DIFF_QK_DIM), const),
        pl.BlockSpec((1, DIFF_QK_DIM), const),
        pl.BlockSpec((1, DIFF_QK_DIM), const),
        pl.BlockSpec((1, DIFF_V_DIM), const),
    ]
    grid_spec = pltpu.PrefetchScalarGridSpec(
        num_scalar_prefetch=1,
        grid=(batch, DIFF_HEADS, nq),
        in_specs=in_specs,
        out_specs=pl.BlockSpec((tq, LANES), qmap),
        scratch_shapes=[
            pltpu.VMEM((2 * tq, 2 * DIFF_V_DIM), F32),
            pltpu.VMEM((2 * tq, 1), F32),
        ],
    )
    return pl.pallas_call(
        functools.partial(_attn_kernel, tq=tq),
        grid_spec=grid_spec,
        out_shape=jax.ShapeDtypeStruct((m, D_ATTN), BF16),
        compiler_params=pltpu.CompilerParams(
            dimension_semantics=("arbitrary", "arbitrary", "arbitrary"), vmem_limit_bytes=VMEM_LIMIT),
        name="attn",
    )(slopes, q, k, v, km, vm, za, lq1, lk1, lq2, lk2, gn)


def _out_kernel(x_ref, ys_ref, ya_ref, w1_ref, w2_ref, g_ref, o_ref):
    hres = (x_ref[...]
            + jnp.dot(ys_ref[...], w1_ref[...], preferred_element_type=F32)
            + jnp.dot(ya_ref[...], w2_ref[...], preferred_element_type=F32))
    ms = jnp.mean(hres * hres, axis=-1, keepdims=True)
    o_ref[...] = hres * lax.rsqrt(ms + EPS) * g_ref[...]


def _out(x2, ys, ya, w1, w2, gf, tm):
    m = x2.shape[0]
    const = lambda i: (0, 0)
    row = lambda i: (i, 0)
    return pl.pallas_call(
        _out_kernel,
        grid=(m // tm,),
        in_specs=[
            pl.BlockSpec((tm, D_MODEL), row),
            pl.BlockSpec((tm, D_SSD), row),
            pl.BlockSpec((tm, D_ATTN), row),
            pl.BlockSpec((D_SSD, D_MODEL), const, pipeline_mode=pl.Buffered(1)),
            pl.BlockSpec((D_ATTN, D_MODEL), const, pipeline_mode=pl.Buffered(1)),
            pl.BlockSpec((1, D_MODEL), const),
        ],
        out_specs=pl.BlockSpec((tm, D_MODEL), row),
        out_shape=jax.ShapeDtypeStruct((m, D_MODEL), F32),
        compiler_params=pltpu.CompilerParams(
            dimension_semantics=("arbitrary",), vmem_limit_bytes=VMEM_LIMIT),
        name="outproj",
    )(x2, ys, ya, w1, w2, gf)


def _pad_rows_top(a, rows):
    return jnp.pad(a, ((rows - a.shape[0], 0), (0, 0)))


def kernel(x, meta, norm_g, w_in, conv_w, conv_b, dt_bias, a_log, d_skip, ssd_norm_g,
           lambda_q1, lambda_k1, lambda_q2, lambda_k2, attn_norm_g, w_out, final_norm_g):
    batch, seq, _ = x.shape
    assert seq % 512 == 0 and x.shape[2] == D_MODEL and meta.shape == (N_META, D_MODEL)
    nc = seq // CHUNK
    layer = 0

    w = w_in[layer]
    o0 = D_SSD
    o1 = o0 + CONV_DIM
    o2 = o1 + SSD_HEADS
    o3 = o2 + D_ATTN
    o4 = o3 + D_ATTN
    o5 = o4 + D_ATTN
    ws = [
        w[:, :o0],
        w[:, o0:o1],
        jnp.pad(w[:, o1:o2], ((0, 0), (0, DT_PAD - SSD_HEADS))),
        w[:, o2:o3] * (DIFF_QK_DIM ** -0.5),
        w[:, o3:o4],
        w[:, o4:o5],
        w[:, o5:],
    ]
    ws = [wi.astype(BF16) for wi in ws]
    g_in = norm_g[layer].reshape(1, D_MODEL)

    x2 = x.reshape(batch * seq, D_MODEL)
    zs, xbc, dtr, q, k, v, za = _proj(x2, g_in, ws, 512)
    _, mxbc, mdtr, _, mk, mv, _ = _proj(meta, g_in, ws, N_META)
    mxbc = _pad_rows_top(mxbc, CHUNK)
    mdtr = _pad_rows_top(mdtr, CHUNK)
    mk = _pad_rows_top(mk, CHUNK)
    mv = _pad_rows_top(mv, CHUNK)

    pad16 = lambda a: jnp.pad(a.reshape(1, SSD_HEADS), ((0, 0), (0, DT_PAD - SSD_HEADS)))
    y_ssd = _ssd(
        xbc, dtr, zs, mxbc, mdtr,
        conv_w[layer], conv_b[layer].reshape(1, CONV_DIM),
        pad16(dt_bias[layer]), pad16(a_log[layer]),
        jnp.repeat(d_skip[layer], SSD_HEADDIM).reshape(1, D_SSD),
        ssd_norm_g[layer].reshape(1, D_SSD), batch, nc)

    slopes = jnp.exp2(-ALIBI_MAX * jnp.arange(1, DIFF_HEADS + 1, dtype=F32) / DIFF_HEADS)
    r64 = lambda a: a[layer].reshape(1, DIFF_QK_DIM)
    y_attn = _attn(slopes, q, k, v, mk, mv, za, r64(lambda_q1), r64(lambda_k1), r64(lambda_q2),
                   r64(lambda_k2), attn_norm_g[layer].reshape(1, DIFF_V_DIM), batch, seq, 256)

    wo = w_out[layer].astype(BF16)
    out = _out(x2, y_ssd, y_attn, wo[:D_SSD], wo[D_SSD:], final_norm_g.reshape(1, D_MODEL), 512)
    return out.reshape(batch, seq, D_MODEL)
```

```python
import functools
import math

import jax
import jax.numpy as jnp
from jax import lax
from jax.experimental import pallas as pl
from jax.experimental.pallas import tpu as pltpu

D_MODEL = 1024
N_META = 16
D_SSD = 1024
SSD_HEADDIM = 64
SSD_HEADS = 16
SSD_GROUPS = 2
D_STATE = 128
CONV_K = 4
CHUNK = 128
D_ATTN = 1024
DIFF_HEADS = 8
DIFF_V_DIM = 128
DIFF_QK_DIM = 64
CONV_DIM = D_SSD + 2 * SSD_GROUPS * D_STATE
N_PAD = CHUNK - N_META
EPS = 1e-5
NEG_INF = -1e30
ALIBI_MAX = 8.0
LAM_INIT = 0.8 - 0.6 * math.exp(-0.3 * 0)

LANES = 128
DT_PAD = LANES
HALO = 8
VMEM_LIMIT = 56 * 1024 * 1024
ATTN_TQ = 512
ATTN_TS = 256

F32 = jnp.float32
BF16 = jnp.bfloat16


def _silu(v):
    return v * jax.nn.sigmoid(v)


def _proj_kernel(x_ref, g_ref, wzs_ref, wxbc_ref, wdt_ref, wq_ref, wk_ref, wv_ref, wza_ref,
                 zs_ref, xbc_ref, dt_ref, q_ref, k_ref, v_ref, za_ref):
    x = x_ref[...]
    ms = jnp.mean(x * x, axis=-1, keepdims=True)
    hn = (x * lax.rsqrt(ms + EPS) * g_ref[...]).astype(BF16)

    def mm(w_ref):
        return jnp.dot(hn, w_ref[...], preferred_element_type=F32)

    zs_ref[...] = mm(wzs_ref)
    xbc_ref[...] = mm(wxbc_ref)
    dt_ref[...] = mm(wdt_ref)
    q_ref[...] = mm(wq_ref).astype(BF16)
    k_ref[...] = mm(wk_ref).astype(BF16)
    v_ref[...] = mm(wv_ref).astype(BF16)
    za_ref[...] = mm(wza_ref)


def _proj(x2, g, ws, tm):
    m = x2.shape[0]
    widths = [w.shape[1] for w in ws]
    dtypes = [F32, F32, F32, BF16, BF16, BF16, F32]
    const = lambda i: (0, 0)
    row = lambda i: (i, 0)
    in_specs = [pl.BlockSpec((tm, D_MODEL), row), pl.BlockSpec((1, D_MODEL), const)]
    in_specs += [pl.BlockSpec((D_MODEL, n), const, pipeline_mode=pl.Buffered(1)) for n in widths]
    out_specs = [pl.BlockSpec((tm, n), row) for n in widths]
    out_shape = [jax.ShapeDtypeStruct((m, n), dt) for n, dt in zip(widths, dtypes)]
    return pl.pallas_call(
        _proj_kernel,
        grid=(m // tm,),
        in_specs=in_specs,
        out_specs=out_specs,
        out_shape=out_shape,
        compiler_params=pltpu.CompilerParams(
            dimension_semantics=("arbitrary",), vmem_limit_bytes=VMEM_LIMIT),
        name="proj",
    )(x2, g, *ws)


def _conv_silu(xs_scr, cw_ref, cb_ref):
    acc = cb_ref[...] + xs_scr[pl.ds(HALO - 3, CHUNK), :] * cw_ref[0:1, :]
    for k in range(1, CONV_K):
        acc = acc + xs_scr[pl.ds(HALO - 3 + k, CHUNK), :] * cw_ref[k:k + 1, :]
    return _silu(acc)


def _softplus(v):
    return jnp.maximum(v, 0.0) + jnp.log1p(jnp.exp(-jnp.abs(v)))


def _pair_expand(v, j, lane_lo):
    a = jnp.broadcast_to(v[:, 2 * j:2 * j + 1], (v.shape[0], LANES))
    b = jnp.broadcast_to(v[:, 2 * j + 1:2 * j + 2], (v.shape[0], LANES))
    return jnp.where(lane_lo, a, b)


def _ssd_chunk(xbc, dt, a_row, s_scr, *, want_y, dskip_ref=None):
    n_pairs = SSD_HEADS // 2
    pairs_per_group = n_pairs // SSD_GROUPS
    gw = pairs_per_group * LANES

    row_i = lax.broadcasted_iota(jnp.int32, (CHUNK, CHUNK), 0)
    col_i = lax.broadcasted_iota(jnp.int32, (CHUNK, CHUNK), 1)
    tri = row_i >= col_i
    lane_lo = col_i < SSD_HEADDIM

    adt = dt * a_row
    acs = jnp.dot(tri.astype(F32), adt, precision=lax.Precision.HIGHEST,
                  preferred_element_type=F32)
    acs_t = acs.T
    eacs = jnp.exp(acs)
    decay_end = jnp.exp(acs[CHUNK - 1:CHUNK, :] - acs)

    ys = []
    for g in range(SSD_GROUPS):
        bg = xbc[:, D_SSD + g * D_STATE:D_SSD + (g + 1) * D_STATE]
        s_prev = s_scr[g]
        xdec = []
        if want_y:
            cg = xbc[:, D_SSD + (SSD_GROUPS + g) * D_STATE:D_SSD + (SSD_GROUPS + g + 1) * D_STATE].astype(BF16)
            cb = lax.dot_general(cg, bg.astype(BF16), (((1,), (1,)), ((), ())),
                                 preferred_element_type=F32)
            y_off = jnp.dot(cg, s_prev.astype(BF16), preferred_element_type=F32)
        for jj in range(pairs_per_group):
            j = g * pairs_per_group + jj
            xp = xbc[:, j * LANES:(j + 1) * LANES]
            xdt = xp * _pair_expand(dt, j, lane_lo)
            xdec.append((xdt * _pair_expand(decay_end, j, lane_lo)).astype(BF16))
            if want_y:
                ms = []
                for h in (2 * j, 2 * j + 1):
                    seg = jnp.exp(jnp.where(tri, acs[:, h:h + 1] - acs_t[h:h + 1, :], -jnp.inf))
                    ms.append((cb * seg).astype(BF16))
                lhs = jnp.concatenate(ms, axis=1)
                rhs = jnp.concatenate([jnp.where(lane_lo, xdt, 0.0), jnp.where(lane_lo, 0.0, xdt)],
                                      axis=0).astype(BF16)
                y_diag = jnp.dot(lhs, rhs, preferred_element_type=F32)
                y = (y_diag + y_off[:, jj * LANES:(jj + 1) * LANES] * _pair_expand(eacs, j, lane_lo)
                     + dskip_ref[:, j * LANES:(j + 1) * LANES] * xp)
                ys.append(y)
        chunk_decay = jnp.concatenate(
            [_pair_expand(eacs, g * pairs_per_group + jj, lane_lo)[CHUNK - 1:CHUNK, :]
             for jj in range(pairs_per_group)], axis=1)
        bg_t = bg.T.astype(BF16)
        s_new = jnp.dot(bg_t, jnp.concatenate(xdec, axis=1), preferred_element_type=F32)
        s_scr[g] = chunk_decay * s_prev + s_new
    return ys


def _ssd_kernel(xbc_ref, dt_ref, z_ref, mxbc_ref, mdt_ref, cw_ref, cb_ref, dtb_ref, alog_ref,
                dskip_ref, g_ref, y_ref, xs_scr, s_scr):
    c = pl.program_id(1)
    a_row = -jnp.exp(alog_ref[...])

    @pl.when(c == 0)
    def _():
        xs_scr[pl.ds(0, HALO), :] = jnp.zeros((HALO, CONV_DIM), F32)
        xs_scr[pl.ds(HALO, CHUNK), :] = mxbc_ref[...]
        valid = lax.broadcasted_iota(jnp.int32, (CHUNK, 1), 0) >= N_PAD
        xbc = jnp.where(valid, _conv_silu(xs_scr, cw_ref, cb_ref), 0.0)
        dt = jnp.where(valid, _softplus(mdt_ref[...] + dtb_ref[...]), 0.0)
        s_scr[...] = jnp.zeros(s_scr.shape, F32)
        _ssd_chunk(xbc, dt, a_row, s_scr, want_y=False)
        xs_scr[pl.ds(0, HALO), :] = xs_scr[pl.ds(CHUNK, HALO), :]

    xs_scr[pl.ds(HALO, CHUNK), :] = xbc_ref[...]
    xbc = _conv_silu(xs_scr, cw_ref, cb_ref)
    dt = _softplus(dt_ref[...] + dtb_ref[...])
    ys = _ssd_chunk(xbc, dt, a_row, s_scr, want_y=True, dskip_ref=dskip_ref)
    xs_scr[pl.ds(0, HALO), :] = xs_scr[pl.ds(CHUNK, HALO), :]

    gw = D_SSD // SSD_GROUPS
    pairs_per_group = gw // LANES
    for g in range(SSD_GROUPS):
        yg = jnp.concatenate(ys[g * pairs_per_group:(g + 1) * pairs_per_group], axis=1)
        yg = yg * _silu(z_ref[:, g * gw:(g + 1) * gw])
        yg = yg * lax.rsqrt(jnp.mean(yg * yg, axis=-1, keepdims=True) + EPS)
        y_ref[:, g * gw:(g + 1) * gw] = (yg * g_ref[:, g * gw:(g + 1) * gw]).astype(BF16)


def _ssd(xbc, dt, zs, mxbc, mdt, cw, cb, dtb, alog, dskip, gn, batch, nc):
    m = xbc.shape[0]
    const = lambda b, c: (0, 0)
    row = lambda b, c: (b * nc + c, 0)
    in_specs = [
        pl.BlockSpec((CHUNK, CONV_DIM), row),
        pl.BlockSpec((CHUNK, DT_PAD), row),
        pl.BlockSpec((CHUNK, D_SSD), row),
        pl.BlockSpec((CHUNK, CONV_DIM), const),
        pl.BlockSpec((CHUNK, DT_PAD), const),
        pl.BlockSpec((CONV_K, CONV_DIM), const),
        pl.BlockSpec((1, CONV_DIM), const),
        pl.BlockSpec((1, DT_PAD), const),
        pl.BlockSpec((1, DT_PAD), const),
        pl.BlockSpec((1, D_SSD), const),
        pl.BlockSpec((1, D_SSD), const),
    ]
    return pl.pallas_call(
        _ssd_kernel,
        grid=(batch, nc),
        in_specs=in_specs,
        out_specs=pl.BlockSpec((CHUNK, D_SSD), row),
        out_shape=jax.ShapeDtypeStruct((m, D_SSD), BF16),
        scratch_shapes=[
            pltpu.VMEM((HALO + CHUNK, CONV_DIM), F32),
            pltpu.VMEM((SSD_GROUPS, D_STATE, D_SSD // SSD_GROUPS), F32),
        ],
        compiler_params=pltpu.CompilerParams(
            dimension_semantics=("arbitrary", "arbitrary"), vmem_limit_bytes=VMEM_LIMIT),
        name="ssd",
    )(xbc, dt, zs, mxbc, mdt, cw, cb, dtb, alog, dskip, gn)


def _attn_kernel(slopes_ref, q_ref, k_ref, v_ref, km_ref, vm_ref, za_ref,
                 lq1_ref, lk1_ref, lq2_ref, lk2_ref, g_ref, o_ref,
                 acc_scr, m_scr, sa_scr, sb_scr, ma_scr, mb_scr, db_scr, *, tq):
    h = pl.program_id(1)
    qi = pl.program_id(2)
    slope = slopes_ref[h]
    q0 = qi * tq
    ts = ATTN_TS
    rows = 2 * tq
    assert tq == 2 * ts

    @pl.when(qi == 0)
    def _():
        r = lax.broadcasted_iota(jnp.int32, (tq, tq), 0)
        c = lax.broadcasted_iota(jnp.int32, (tq, tq), 1)
        db_scr[...] = jnp.where(c <= r, slope * c.astype(F32), NEG_INF)

    qf = q_ref[...].astype(F32)
    lane = lax.broadcasted_iota(jnp.int32, (tq, LANES), 1)
    qs = jnp.concatenate([jnp.where(lane < DIFF_QK_DIM, qf, 0.0),
                          jnp.where(lane < DIFF_QK_DIM, 0.0, qf)], axis=0).astype(BF16)

    def scores(k_t, bias):
        s = lax.dot_general(qs, k_t, (((1,), (1,)), ((), ())), preferred_element_type=F32)
        return s + bias

    def rowmax(s):
        return jnp.broadcast_to(jnp.max(s, axis=1, keepdims=True), (rows, LANES))

    def wide(m, n):
        return jnp.concatenate([m] * (n // LANES), axis=1)

    def v_aug(v_t):
        return jnp.concatenate([v_t, jnp.ones_like(v_t)], axis=1)

    def produce(k0, bias, s_scr, mx_scr):
        s = scores(k_ref[pl.ds(k0, ts), :], bias)
        s_scr[...] = s
        mx_scr[...] = rowmax(s)

    def consume(k0, s_scr, mx_scr):
        m_prev = m_scr[...]
        m_new = jnp.maximum(m_prev, mx_scr[...])
        alpha = jnp.exp(m_prev - m_new)
        p = jnp.exp(s_scr[...] - wide(m_new, ts)).astype(BF16)
        pv = jnp.dot(p, v_aug(v_ref[pl.ds(k0, ts), :]), preferred_element_type=F32)
        acc_scr[...] = wide(alpha, 2 * DIFF_V_DIM) * acc_scr[...] + pv
        m_scr[...] = m_new

    cm = lax.broadcasted_iota(jnp.int32, (1, CHUNK), 1)
    bias_m = jnp.where(cm >= N_PAD, slope * (cm - CHUNK - q0).astype(F32), NEG_INF)
    s_m = scores(km_ref[...], bias_m)
    m_m = rowmax(s_m)
    p_m = jnp.exp(s_m - m_m).astype(BF16)
    acc_scr[...] = jnp.dot(p_m, v_aug(vm_ref[...]), preferred_element_type=F32)
    m_scr[...] = m_m

    ck = lax.broadcasted_iota(jnp.int32, (1, ts), 1)

    def row_bias(k0):
        return slope * (ck + (k0 - q0)).astype(F32)

    def diag_bias(half):
        d = db_scr[:, half * ts:(half + 1) * ts]
        return jnp.concatenate([d, d], axis=0)

    kda = pl.multiple_of(q0, tq)
    kdb = pl.multiple_of(q0 + ts, ts)
    produce(kda, diag_bias(0), sa_scr, ma_scr)
    produce(kdb, diag_bias(1), sb_scr, mb_scr)
    consume(kda, sa_scr, ma_scr)

    @pl.when(qi > 0)
    def _():
        produce(0, row_bias(0), sa_scr, ma_scr)

    consume(kdb, sb_scr, mb_scr)

    def body(j, carry):
        ka = pl.multiple_of(j * tq, tq)
        kb = pl.multiple_of(j * tq + ts, ts)
        kn = pl.multiple_of(j * tq + tq, tq)
        produce(kb, row_bias(kb), sb_scr, mb_scr)
        consume(ka, sa_scr, ma_scr)
        produce(kn, row_bias(kn), sa_scr, ma_scr)
        consume(kb, sb_scr, mb_scr)
        return carry

    lax.fori_loop(0, qi - 1, body, 0)

    @pl.when(qi > 0)
    def _():
        ka = pl.multiple_of((qi - 1) * tq, tq)
        kb = pl.multiple_of((qi - 1) * tq + ts, ts)
        produce(kb, row_bias(kb), sb_scr, mb_scr)
        consume(ka, sa_scr, ma_scr)
        consume(kb, sb_scr, mb_scr)

    acc = acc_scr[...]
    o1 = acc[:tq, :DIFF_V_DIM] / acc[:tq, DIFF_V_DIM:]
    o2 = acc[tq:, :DIFF_V_DIM] / acc[tq:, DIFF_V_DIM:]
    lam = (jnp.exp(jnp.sum(lq1_ref[...] * lk1_ref[...], axis=-1, keepdims=True))
           - jnp.exp(jnp.sum(lq2_ref[...] * lk2_ref[...], axis=-1, keepdims=True)) + LAM_INIT)
    o = o1 - lam * o2
    o = o * lax.rsqrt(jnp.mean(o * o, axis=-1, keepdims=True) + EPS) * g_ref[...]
    o = o * (1.0 - LAM_INIT)
    o_ref[...] = (o * _silu(za_ref[...])).astype(BF16)


def _attn(slopes, q, k, v, km, vm, za, lq1, lk1, lq2, lk2, gn, batch, seq, tq):
    nq = seq // tq
    m = q.shape[0]
    qmap = lambda b, h, i, *_: (b * nq + i, h)
    kvmap = lambda b, h, i, *_: (b, h)
    hmap = lambda b, h, i, *_: (0, h)
    const = lambda b, h, i, *_: (0, 0)
    in_specs = [
        pl.BlockSpec((tq, LANES), qmap),
        pl.BlockSpec((seq, LANES), kvmap),
        pl.BlockSpec((seq, LANES), kvmap),
        pl.BlockSpec((CHUNK, LANES), hmap),
        pl.BlockSpec((CHUNK, LANES), hmap),
        pl.BlockSpec((tq, LANES), qmap),
        pl.BlockSpec((1, DIFF_QK_DIM), const),
        pl.BlockSpec((1, DIFF_QK_DIM), const),
        pl.BlockSpec((1, DIFF_QK_DIM), const),
        pl.BlockSpec((1, DIFF_QK_DIM), const),
        pl.BlockSpec((1, DIFF_V_DIM), const),
    ]
    grid_spec = pltpu.PrefetchScalarGridSpec(
        num_scalar_prefetch=1,
        grid=(batch, DIFF_HEADS, nq),
        in_specs=in_specs,
        out_specs=pl.BlockSpec((tq, LANES), qmap),
        scratch_shapes=[
            pltpu.VMEM((2 * tq, 2 * DIFF_V_DIM), F32),
            pltpu.VMEM((2 * tq, LANES), F32),
            pltpu.VMEM((2 * tq, ATTN_TS), F32),
            pltpu.VMEM((2 * tq, ATTN_TS), F32),
            pltpu.VMEM((2 * tq, LANES), F32),
            pltpu.VMEM((2 * tq, LANES), F32),
            pltpu.VMEM((tq, tq), F32),
        ],
    )
    return pl.pallas_call(
        functools.partial(_attn_kernel, tq=tq),
        grid_spec=grid_spec,
        out_shape=jax.ShapeDtypeStruct((m, D_ATTN), BF16),
        compiler_params=pltpu.CompilerParams(
            dimension_semantics=("arbitrary", "arbitrary", "arbitrary"), vmem_limit_bytes=VMEM_LIMIT),
        name="attn",
    )(slopes, q, k, v, km, vm, za, lq1, lk1, lq2, lk2, gn)


def _out_kernel(x_ref, ys_ref, ya_ref, w1_ref, w2_ref, g_ref, o_ref):
    hres = (x_ref[...]
            + jnp.dot(ys_ref[...], w1_ref[...], preferred_element_type=F32)
            + jnp.dot(ya_ref[...], w2_ref[...], preferred_element_type=F32))
    ms = jnp.mean(hres * hres, axis=-1, keepdims=True)
    o_ref[...] = hres * lax.rsqrt(ms + EPS) * g_ref[...]


def _out(x2, ys, ya, w1, w2, gf, tm):
    m = x2.shape[0]
    const = lambda i: (0, 0)
    row = lambda i: (i, 0)
    return pl.pallas_call(
        _out_kernel,
        grid=(m // tm,),
        in_specs=[
            pl.BlockSpec((tm, D_MODEL), row),
            pl.BlockSpec((tm, D_SSD), row),
            pl.BlockSpec((tm, D_ATTN), row),
            pl.BlockSpec((D_SSD, D_MODEL), const, pipeline_mode=pl.Buffered(1)),
            pl.BlockSpec((D_ATTN, D_MODEL), const, pipeline_mode=pl.Buffered(1)),
            pl.BlockSpec((1, D_MODEL), const),
        ],
        out_specs=pl.BlockSpec((tm, D_MODEL), row),
        out_shape=jax.ShapeDtypeStruct((m, D_MODEL), F32),
        compiler_params=pltpu.CompilerParams(
            dimension_semantics=("arbitrary",), vmem_limit_bytes=VMEM_LIMIT),
        name="outproj",
    )(x2, ys, ya, w1, w2, gf)


def _pad_rows_top(a, rows):
    return jnp.pad(a, ((rows - a.shape[0], 0), (0, 0)))


def kernel(x, meta, norm_g, w_in, conv_w, conv_b, dt_bias, a_log, d_skip, ssd_norm_g,
           lambda_q1, lambda_k1, lambda_q2, lambda_k2, attn_norm_g, w_out, final_norm_g):
    batch, seq, _ = x.shape
    assert seq % 512 == 0 and x.shape[2] == D_MODEL and meta.shape == (N_META, D_MODEL)
    nc = seq // CHUNK
    layer = 0

    w = w_in[layer]
    o0 = D_SSD
    o1 = o0 + CONV_DIM
    o2 = o1 + SSD_HEADS
    o3 = o2 + D_ATTN
    o4 = o3 + D_ATTN
    o5 = o4 + D_ATTN
    ws = [
        w[:, :o0],
        w[:, o0:o1],
        jnp.pad(w[:, o1:o2], ((0, 0), (0, DT_PAD - SSD_HEADS))),
        w[:, o2:o3] * (DIFF_QK_DIM ** -0.5),
        w[:, o3:o4],
        w[:, o4:o5],
        w[:, o5:],
    ]
    ws = [wi.astype(BF16) for wi in ws]
    g_in = norm_g[layer].reshape(1, D_MODEL)

    x2 = x.reshape(batch * seq, D_MODEL)
    zs, xbc, dtr, q, k, v, za = _proj(x2, g_in, ws, 512)
    _, mxbc, mdtr, _, mk, mv, _ = _proj(meta, g_in, ws, N_META)
    mxbc = _pad_rows_top(mxbc, CHUNK)
    mdtr = _pad_rows_top(mdtr, CHUNK)
    mk = _pad_rows_top(mk, CHUNK)
    mv = _pad_rows_top(mv, CHUNK)

    pad16 = lambda a: jnp.pad(a.reshape(1, SSD_HEADS), ((0, 0), (0, DT_PAD - SSD_HEADS)))
    y_ssd = _ssd(
        xbc, dtr, zs, mxbc, mdtr,
        conv_w[layer], conv_b[layer].reshape(1, CONV_DIM),
        pad16(dt_bias[layer]), pad16(a_log[layer]),
        jnp.repeat(d_skip[layer], SSD_HEADDIM).reshape(1, D_SSD),
        ssd_norm_g[layer].reshape(1, D_SSD), batch, nc)

    slopes = jnp.exp2(-ALIBI_MAX * jnp.arange(1, DIFF_HEADS + 1, dtype=F32) / DIFF_HEADS)
    r64 = lambda a: a[layer].reshape(1, DIFF_QK_DIM)
    y_attn = _attn(slopes, q, k, v, mk, mv, za, r64(lambda_q1), r64(lambda_k1), r64(lambda_q2),
                   r64(lambda_k2), attn_norm_g[layer].reshape(1, DIFF_V_DIM), batch, seq, ATTN_TQ)

    wo = w_out[layer].astype(BF16)
    out = _out(x2, y_ssd, y_attn, wo[:D_SSD], wo[D_SSD:], final_norm_g.reshape(1, D_MODEL), 512)
    return out.reshape(batch, seq, D_MODEL)
```
